```python
import math
import jax, jax.numpy as jnp
from jax import lax
import numpy as np

D_MODEL = 1024
BATCH = 4
SEQ = 8192
DEPTH = 2

N_HEADS_ATTN = 8
HEAD_DIM_ATTN = 64
V_DIM_ATTN = 2 * HEAD_DIM_ATTN
ATTN_WIDTH = N_HEADS_ATTN * V_DIM_ATTN
Q_BLOCK = 128
SSM_WIDTH = D_MODEL
SSM_HEADDIM = 64
N_HEADS_SSM = SSM_WIDTH // SSM_HEADDIM
SSM_GROUPS = 4
SSM_HEADS_PER_GROUP = N_HEADS_SSM // SSM_GROUPS
SSM_STATE = 128
CONV_WIDTH = 5
CONV_CH = SSM_WIDTH + 2 * SSM_GROUPS * SSM_STATE
CHUNK = 128
D_MIX = ATTN_WIDTH + SSM_WIDTH
Q_COLS = N_HEADS_ATTN * 2 * HEAD_DIM_ATTN
K_COLS = N_HEADS_ATTN * 2 * HEAD_DIM_ATTN
V_COLS = ATTN_WIDTH
Z_COLS = SSM_WIDTH
XBC_COLS = CONV_CH
DT_COLS = N_HEADS_SSM
IN_COLS = Q_COLS + K_COLS + V_COLS + Z_COLS + XBC_COLS + DT_COLS
D_FF = -(-8 * D_MODEL // (3 * 256)) * 256
DEEPNORM_ALPHA = (2 * DEPTH) ** 0.25
DEEPNORM_BETA = (8 * DEPTH) ** -0.25
LN_EPS = 1e-5
RMS_EPS = 1e-5

kernel_name = "hymba_diffattn_ssd_deepnorm_encoder"


def layernorm(x, g, b):
    x32 = x.astype(jnp.float32)
    mu = jnp.mean(x32, axis=-1, keepdims=True)
    var = jnp.mean(jnp.square(x32 - mu), axis=-1, keepdims=True)
    y = (x32 - mu) * lax.rsqrt(var + LN_EPS)
    return (y * g.astype(jnp.float32) + b.astype(jnp.float32)).astype(x.dtype)


def rmsnorm(x, w):
    x32 = x.astype(jnp.float32)
    y = x32 * lax.rsqrt(jnp.mean(jnp.square(x32), axis=-1, keepdims=True) + RMS_EPS)
    return (y * w.astype(jnp.float32)).astype(x.dtype)


def alibi_slopes(n_heads):
    return jnp.exp2(-8.0 * jnp.arange(1, n_heads + 1, dtype=jnp.float32) / n_heads)


def diff_attention(q, k, v, lq1, lk1, lq2, lk2, subln_w, lambda_init):
    b, S = q.shape[0], q.shape[1]
    nblk = S // Q_BLOCK
    f32 = jnp.float32
    lam = (jnp.exp(jnp.sum(lq1.astype(f32) * lk1.astype(f32)))
           - jnp.exp(jnp.sum(lq2.astype(f32) * lk2.astype(f32))) + lambda_init)
    slopes = alibi_slopes(N_HEADS_ATTN)
    k_t = k.transpose(0, 2, 3, 1, 4)
    v_t = v.transpose(0, 2, 1, 3)
    q_blocks = q.reshape(b, nblk, Q_BLOCK, N_HEADS_ATTN, 2, HEAD_DIM_ATTN)
    q_blocks = q_blocks.transpose(1, 0, 3, 4, 2, 5)
    pos_k = jnp.arange(S, dtype=jnp.int32)
    starts = jnp.arange(nblk, dtype=jnp.int32) * Q_BLOCK
    scale = HEAD_DIM_ATTN ** -0.5

    def block(args):
        qb, start = args
        s = jnp.einsum('bhcqd,bhckd->bhcqk', qb, k_t).astype(f32) * scale
        pos_q = start + jnp.arange(Q_BLOCK, dtype=jnp.int32)
        dist = jnp.abs(pos_q[:, None] - pos_k[None, :]).astype(f32)
        s = s - slopes[:, None, None, None] * dist
        p = jax.nn.softmax(s, axis=-1)
        a = p[:, :, 0] - lam * p[:, :, 1]
        return jnp.einsum('bhqk,bhkd->bhqd', a.astype(v_t.dtype), v_t)

    o = lax.map(block, (q_blocks, starts))
    o = o.transpose(1, 0, 3, 2, 4).reshape(b, S, N_HEADS_ATTN, V_DIM_ATTN)
    o = rmsnorm(o, subln_w) * (1.0 - lambda_init)
    return o.reshape(b, S, ATTN_WIDTH)


def depthwise_conv_centred(u, w, bias):
    out = lax.conv_general_dilated(
        u, w[:, None, :].astype(u.dtype), window_strides=(1,),
        padding=[(CONV_WIDTH // 2, CONV_WIDTH // 2)],
        dimension_numbers=('NWC', 'WIO', 'NWC'),
        feature_group_count=u.shape[-1])
    return out + bias.astype(out.dtype)


def segsum_exp(cs):
    T = cs.shape[-1]
    diff = cs[..., :, None] - cs[..., None, :]
    mask = jnp.tril(jnp.ones((T, T), dtype=bool))
    return jnp.exp(jnp.where(mask, diff, -jnp.inf))


def ssd_scan(xh, dt, A, Bg, Cg):
    b, S, G, R, P = xh.shape
    nc = S // CHUNK
    dA = dt * A
    xc = (xh * dt[..., None].astype(xh.dtype)).reshape(b, nc, CHUNK, G, R, P)
    Bc = Bg.reshape(b, nc, CHUNK, G, SSM_STATE)
    Cc = Cg.reshape(b, nc, CHUNK, G, SSM_STATE)
    acs = jnp.cumsum(dA.reshape(b, nc, CHUNK, G, R).transpose(0, 3, 4, 1, 2), axis=-1)
    L = segsum_exp(acs)
    cb = jnp.einsum('bclgn,bcsgn->bcgls', Cc, Bc)
    y_diag = jnp.einsum('bcgls,bgrcls,bcsgrp->bclgrp', cb, L, xc)
    decay_states = jnp.exp(acs[..., -1:] - acs)
    states = jnp.einsum('bclgn,bgrcl,bclgrp->bcgrpn', Bc, decay_states, xc)
    states = jnp.concatenate([jnp.zeros_like(states[:, :1]), states], axis=1)
    chunk_tot = jnp.pad(acs[..., -1], ((0, 0), (0, 0), (0, 0), (1, 0)))
    decay_chunk = segsum_exp(jnp.cumsum(chunk_tot, axis=-1))
    new_states = jnp.einsum('bgrzc,bcgrpn->bzgrpn', decay_chunk, states)
    states = new_states[:, :-1]
    y_off = jnp.einsum('bclgn,bcgrpn,bgrcl->bclgrp', Cc, states, jnp.exp(acs))
    return (y_diag + y_off).reshape(b, S, G, R, P)


def ssd_mixer(z, xbc, dt_raw, conv_w, conv_b, dt_bias_fwd, dt_bias_bwd,
              a_log_fwd, a_log_bwd, d_skip, norm_w):
    b, S = z.shape[0], z.shape[1]
    xbc = jax.nn.silu(depthwise_conv_centred(xbc, conv_w, conv_b))
    xs = xbc[..., :SSM_WIDTH]
    Bg = xbc[..., SSM_WIDTH:SSM_WIDTH + SSM_GROUPS * SSM_STATE].reshape(b, S, SSM_GROUPS, SSM_STATE)
    Cg = xbc[..., SSM_WIDTH + SSM_GROUPS * SSM_STATE:].reshape(b, S, SSM_GROUPS, SSM_STATE)
    xh = xs.reshape(b, S, SSM_GROUPS, SSM_HEADS_PER_GROUP, SSM_HEADDIM)
    dt32 = dt_raw.astype(jnp.float32)

    def direction(xh_d, B_d, C_d, dt_d, dt_bias, a_log):
        dt = jax.nn.softplus(dt_d + dt_bias.astype(jnp.float32))
        dt = dt.reshape(b, S, SSM_GROUPS, SSM_HEADS_PER_GROUP)
        A = -jnp.exp(a_log.astype(jnp.float32)).reshape(SSM_GROUPS, SSM_HEADS_PER_GROUP)
        return ssd_scan(xh_d, dt, A, B_d, C_d)

    flip = lambda t: jnp.flip(t, axis=1)
    y_f = direction(xh, Bg, Cg, dt32, dt_bias_fwd, a_log_fwd)
    y_b = flip(direction(flip(xh), flip(Bg), flip(Cg), flip(dt32), dt_bias_bwd, a_log_bwd))
    D = d_skip.reshape(SSM_GROUPS, SSM_HEADS_PER_GROUP)[..., None]
    y = (y_f + y_b + D * xh).astype(z.dtype).reshape(b, S, SSM_WIDTH)
    g = (y * jax.nn.silu(z)).reshape(b, S, SSM_GROUPS, SSM_WIDTH // SSM_GROUPS)
    g32 = g.astype(jnp.float32)
    g32 = g32 * lax.rsqrt(jnp.mean(jnp.square(g32), axis=-1, keepdims=True) + RMS_EPS)
    return (g32.reshape(b, S, SSM_WIDTH) * norm_w.astype(jnp.float32)).astype(z.dtype)


def setup_inputs(seed: int = 0) -> dict:
    key = jax.random.key(seed)
    ks = jax.random.split(key, 24)
    f32 = jnp.float32

    def nrm(k, shape, scale):
        return jax.random.normal(k, shape, f32) * scale

    def dt_bias(k):
        dt = jnp.exp(jax.random.uniform(k, (DEPTH, N_HEADS_SSM), f32, math.log(1e-3), math.log(1e-1)))
        return dt + jnp.log(-jnp.expm1(-dt))

    def a_log(k):
        return jnp.log(jax.random.uniform(k, (DEPTH, N_HEADS_SSM), f32, 1.0, 16.0))

    return {
        "x": nrm(ks[0], (BATCH, SEQ, D_MODEL), 1.0),
        "w_in": nrm(ks[1], (DEPTH, D_MODEL, IN_COLS), D_MODEL ** -0.5),
        "lambda_q1": nrm(ks[2], (DEPTH, HEAD_DIM_ATTN), 0.1),
        "lambda_k1": nrm(ks[3], (DEPTH, HEAD_DIM_ATTN), 0.1),
        "lambda_q2": nrm(ks[4], (DEPTH, HEAD_DIM_ATTN), 0.1),
        "lambda_k2": nrm(ks[5], (DEPTH, HEAD_DIM_ATTN), 0.1),
        "subln_w": 1.0 + nrm(ks[6], (DEPTH, V_DIM_ATTN), 0.02),
        "conv_w": nrm(ks[7], (DEPTH, CONV_WIDTH, CONV_CH), CONV_WIDTH ** -0.5),
        "conv_b": nrm(ks[8], (DEPTH, CONV_CH), 0.02),
        "dt_bias_fwd": dt_bias(ks[9]),
        "dt_bias_bwd": dt_bias(ks[10]),
        "a_log_fwd": a_log(ks[11]),
        "a_log_bwd": a_log(ks[12]),
        "d_skip": 1.0 + nrm(ks[13], (DEPTH, N_HEADS_SSM), 0.02),
        "ssm_norm_w": 1.0 + nrm(ks[14], (DEPTH, SSM_WIDTH), 0.02),
        "w_out": nrm(ks[15], (DEPTH, D_MIX, D_MODEL), D_MIX ** -0.5 * DEEPNORM_BETA),
        "ln1_g": 1.0 + nrm(ks[16], (DEPTH, D_MODEL), 0.02),
        "ln1_b": nrm(ks[17], (DEPTH, D_MODEL), 0.02),
        "w_gate": nrm(ks[18], (DEPTH, D_MODEL, D_FF), D_MODEL ** -0.5),
        "w_up": nrm(ks[19], (DEPTH, D_MODEL, D_FF), D_MODEL ** -0.5),
        "w_down": nrm(ks[20], (DEPTH, D_FF, D_MODEL), D_FF ** -0.5 * DEEPNORM_BETA),
        "ln2_g": 1.0 + nrm(ks[21], (DEPTH, D_MODEL), 0.02),
        "ln2_b": nrm(ks[22], (DEPTH, D_MODEL), 0.02),
    }


def reference(x, w_in, lambda_q1, lambda_k1, lambda_q2, lambda_k2, subln_w, conv_w, conv_b,
              dt_bias_fwd, dt_bias_bwd, a_log_fwd, a_log_bwd, d_skip, ssm_norm_w, w_out,
              ln1_g, ln1_b, w_gate, w_up, w_down, ln2_g, ln2_b):
    b, S = x.shape[0], x.shape[1]
    splits = np.cumsum([Q_COLS, K_COLS, V_COLS, Z_COLS, XBC_COLS]).tolist()
    for l in range(DEPTH):
        lambda_init = 0.8 - 0.6 * math.exp(-0.3 * l)
        h = jnp.einsum('bsd,de->bse', x, w_in[l])
        q, k, v, z, xbc, dt_raw = jnp.split(h, splits, axis=-1)
        attn_out = diff_attention(
            q.reshape(b, S, N_HEADS_ATTN, 2, HEAD_DIM_ATTN),
            k.reshape(b, S, N_HEADS_ATTN, 2, HEAD_DIM_ATTN),
            v.reshape(b, S, N_HEADS_ATTN, V_DIM_ATTN),
            lambda_q1[l], lambda_k1[l], lambda_q2[l], lambda_k2[l], subln_w[l], lambda_init)
        ssd_out = ssd_mixer(z, xbc, dt_raw, conv_w[l], conv_b[l], dt_bias_fwd[l], dt_bias_bwd[l],
                            a_log_fwd[l], a_log_bwd[l], d_skip[l], ssm_norm_w[l])
        mix = jnp.einsum('bse,ed->bsd', jnp.concatenate([attn_out, ssd_out], axis=-1), w_out[l])
        x = layernorm(DEEPNORM_ALPHA * x + mix, ln1_g[l], ln1_b[l])
        hid = jax.nn.silu(jnp.einsum('bsd,df->bsf', x, w_gate[l])) * jnp.einsum('bsd,df->bsf', x, w_up[l])
        ffn = jnp.einsum('bsf,fd->bsd', hid, w_down[l])
        x = layernorm(DEEPNORM_ALPHA * x + ffn, ln2_g[l], ln2_b[l])
    return x
```

```python
import functools
import math

import jax
import jax.numpy as jnp
from jax import lax
from jax.experimental import pallas as pl
from jax.experimental.pallas import tpu as pltpu

F32 = jnp.float32
BF16 = jnp.bfloat16

D_MODEL = 1024
DEPTH = 2
N_HEADS_ATTN = 8
HEAD_DIM_ATTN = 64
V_DIM_ATTN = 128
ATTN_WIDTH = 1024
SSM_WIDTH = 1024
SSM_HEADDIM = 64
N_HEADS_SSM = 16
SSM_GROUPS = 4
SSM_HEADS_PER_GROUP = 4
SSM_STATE = 128
CONV_WIDTH = 5
CONV_CH = 2048
CHUNK = 128
GROUP_WIDTH = SSM_WIDTH // SSM_GROUPS
MAIN_COLS = 6144
Z_COL0 = 3072
XBC_COL0 = 4096
D_FF = 2816
DEEPNORM_ALPHA = (2 * DEPTH) ** 0.25
LN_EPS = 1e-5
RMS_EPS = 1e-5
LOG2E = 1.4426950408889634

LANES = 128
HALO = 16
VMEM_LIMIT = 56 * 1024 * 1024


def _cparams(sem):
    return pltpu.CompilerParams(dimension_semantics=sem, vmem_limit_bytes=VMEM_LIMIT)


def _inproj_kernel(x_ref, w_ref, wdt_ref, h_ref, dt_ref):
    xb = x_ref[...].astype(BF16)
    h_ref[...] = jnp.dot(xb, w_ref[...], preferred_element_type=F32).astype(h_ref.dtype)

    @pl.when(pl.program_id(1) == 0)
    def _():
        dt_ref[...] = jnp.dot(xb, wdt_ref[...], preferred_element_type=F32)


def _inproj(x2d, w_main, w_dt):
    n = x2d.shape[0]
    tm = min(1024, n)
    tn = 1536
    return pl.pallas_call(
        _inproj_kernel,
        grid=(n // tm, MAIN_COLS // tn),
        in_specs=[
            pl.BlockSpec((tm, D_MODEL), lambda i, j: (i, 0)),
            pl.BlockSpec((D_MODEL, tn), lambda i, j: (0, j)),
            pl.BlockSpec((D_MODEL, SSM_GROUPS * LANES), lambda i, j: (0, 0)),
        ],
        out_specs=[
            pl.BlockSpec((tm, tn), lambda i, j: (i, j)),
            pl.BlockSpec((tm, SSM_GROUPS * LANES), lambda i, j: (i, 0)),
        ],
        out_shape=[
            jax.ShapeDtypeStruct((n, MAIN_COLS), BF16),
            jax.ShapeDtypeStruct((n, SSM_GROUPS * LANES), F32),
        ],
        compiler_params=_cparams(("parallel", "arbitrary")),
        name="inproj",
    )(x2d, w_main, w_dt)


def _attn_kernel(lam_ref, subw_ref, ns_ref, qT_ref, k_ref, vT_ref, o_ref,
                 w_sc, m_sc, l_sc, acc_sc, *, tq, tk, nk, lambda_init):
    qi = pl.program_id(2)
    qT = qT_ref[...].astype(F32)
    row = lax.broadcasted_iota(jnp.int32, (2 * HEAD_DIM_ATTN, tq), 0)
    w_sc[:, :tq] = jnp.where(row < HEAD_DIM_ATTN, qT, 0.0).astype(BF16)
    w_sc[:, tq:] = jnp.where(row >= HEAD_DIM_ATTN, qT, 0.0).astype(BF16)
    m_sc[...] = jnp.full(m_sc.shape, -jnp.inf, F32)
    l_sc[...] = jnp.zeros(l_sc.shape, F32)
    acc_sc[...] = jnp.zeros(acc_sc.shape, F32)
    neg_slope = ns_ref[0:1, 0:1]
    key_minus_query = (lax.broadcasted_iota(jnp.int32, (tk, tq), 0)
                       - lax.broadcasted_iota(jnp.int32, (tk, tq), 1))

    def body(ki, carry):
        k0 = pl.multiple_of(ki * tk, tk)
        s = jnp.dot(k_ref[pl.ds(k0, tk), :], w_sc[...], preferred_element_type=F32)
        dist = jnp.abs(key_minus_query + (ki * tk - qi * tq)).astype(F32)
        bias = dist * neg_slope
        s = s + jnp.concatenate([bias, bias], axis=1)
        m_prev = m_sc[...]
        m_new = jnp.maximum(m_prev, jnp.max(s, axis=0, keepdims=True))
        alpha = jnp.exp2(m_prev - m_new)
        p = jnp.exp2(s - m_new)
        l_sc[...] = alpha * l_sc[...] + jnp.sum(p, axis=0, keepdims=True)
        acc_sc[...] = acc_sc[...] * alpha + jnp.dot(vT_ref[ki], p.astype(BF16),
                                                    preferred_element_type=F32)
        m_sc[...] = m_new
        return carry

    lax.fori_loop(0, nk, body, 0)

    l = l_sc[...]
    acc = acc_sc[...]
    o1 = acc[:, :tq] / l[:, :tq]
    o2 = acc[:, tq:] / l[:, tq:]
    lam = (jnp.exp(jnp.sum(lam_ref[0:1, :] * lam_ref[1:2, :], axis=1, keepdims=True))
           - jnp.exp(jnp.sum(lam_ref[2:3, :] * lam_ref[3:4, :], axis=1, keepdims=True))
           + lambda_init)
    o = o1 - lam * o2
    ms = jnp.mean(o * o, axis=0, keepdims=True)
    y = o * lax.rsqrt(ms + RMS_EPS) * subw_ref[...] * (1.0 - lambda_init)
    o_ref[...] = y.T.astype(o_ref.dtype)


def _attention(h3, qT, vT, lam_params, subw_col, neg_slopes, lambda_init, tq, tk):
    b, s, _ = h3.shape
    nk = s // tk
    kern = functools.partial(_attn_kernel, tq=tq, tk=tk, nk=nk, lambda_init=lambda_init)
    return pl.pallas_call(
        kern,
        grid=(b, N_HEADS_ATTN, s // tq),
        in_specs=[
            pl.BlockSpec((4, HEAD_DIM_ATTN), lambda bi, hi, qi: (0, 0)),
            pl.BlockSpec((V_DIM_ATTN, 1), lambda bi, hi, qi: (0, 0)),
            pl.BlockSpec((None, 1, LANES), lambda bi, hi, qi: (hi, 0, 0)),
            pl.BlockSpec((None, None, 2 * HEAD_DIM_ATTN, tq), lambda bi, hi, qi: (bi, hi, 0, qi)),
            pl.BlockSpec((None, s, LANES), lambda bi, hi, qi: (bi, 0, N_HEADS_ATTN + hi)),
            pl.BlockSpec((None, None, nk, V_DIM_ATTN, tk), lambda bi, hi, qi: (bi, hi, 0, 0, 0)),
        ],
        out_specs=pl.BlockSpec((None, tq, V_DIM_ATTN), lambda bi, hi, qi: (bi, qi, hi)),
        out_shape=jax.ShapeDtypeStruct((b, s, ATTN_WIDTH), BF16),
        scratch_shapes=[
            pltpu.VMEM((2 * HEAD_DIM_ATTN, 2 * tq), BF16),
            pltpu.VMEM((1, 2 * tq), F32),
            pltpu.VMEM((1, 2 * tq), F32),
            pltpu.VMEM((V_DIM_ATTN, 2 * tq), F32),
        ],
        compiler_params=_cparams(("parallel", "parallel", "arbitrary")),
        name="diff_attn",
    )(lam_params, subw_col, neg_slopes, qT, h3, vT)


def _conv_kernel(prev_ref, main_ref, next_ref, w_ref, b_ref, o_ref, ext_sc, *, ts, nt):
    ti = pl.program_id(1)
    prev = prev_ref[...].astype(F32)
    nxt = next_ref[...].astype(F32)
    ext_sc[0:HALO, :] = jnp.where(ti > 0, prev, 0.0)
    ext_sc[HALO:HALO + ts, :] = main_ref[...].astype(F32)
    ext_sc[HALO + ts:, :] = jnp.where(ti < nt - 1, nxt, 0.0)
    acc = b_ref[...] + jnp.zeros((ts, b_ref.shape[1]), F32)
    for tap in range(CONV_WIDTH):
        start = HALO + tap - CONV_WIDTH // 2
        acc = acc + w_ref[tap:tap + 1, :] * ext_sc[start:start + ts, :]
    o_ref[...] = (acc / (1.0 + jnp.exp(-acc))).astype(o_ref.dtype)


def _conv_silu(h3, conv_w, conv_b_row):
    b, s, _ = h3.shape
    ts = min(1024, s)
    nt = s // ts
    tc = 512
    hb = ts // HALO
    col0 = XBC_COL0 // tc
    kern = functools.partial(_conv_kernel, ts=ts, nt=nt)
    return pl.pallas_call(
        kern,
        grid=(b, nt, CONV_CH // tc),
        in_specs=[
            pl.BlockSpec((None, HALO, tc), lambda bi, ti, ci: (bi, jnp.maximum(ti * hb - 1, 0), col0 + ci)),
            pl.BlockSpec((None, ts, tc), lambda bi, ti, ci: (bi, ti, col0 + ci)),
            pl.BlockSpec((None, HALO, tc),
                         lambda bi, ti, ci: (bi, jnp.minimum((ti + 1) * hb, s // HALO - 1), col0 + ci)),
            pl.BlockSpec((CONV_WIDTH, tc), lambda bi, ti, ci: (0, ci)),
            pl.BlockSpec((1, tc), lambda bi, ti, ci: (0, ci)),
        ],
        out_specs=pl.BlockSpec((None, ts, tc), lambda bi, ti, ci: (bi, ti, ci)),
        out_shape=jax.ShapeDtypeStruct((b, s, CONV_CH), BF16),
        scratch_shapes=[pltpu.VMEM((ts + 2 * HALO, tc), F32)],
        compiler_params=_cparams(("parallel", "parallel", "parallel")),
        name="conv_silu",
    )(h3, h3, h3, conv_w, conv_b_row)


def _softplus(x):
    return jnp.maximum(x, 0.0) + jnp.log1p(jnp.exp(-jnp.abs(x)))


def _split_dot(tri, v):
    hi = v.astype(BF16)
    r1 = v - hi.astype(F32)
    mid = r1.astype(BF16)
    lo = (r1 - mid.astype(F32)).astype(BF16)
    return (jnp.dot(tri, hi, preferred_element_type=F32)
            + jnp.dot(tri, mid, preferred_element_type=F32)
            + jnp.dot(tri, lo, preferred_element_type=F32))


def _per_head_lanes(cols, head_of_lane):
    out = cols[:, 3:4]
    for r in (2, 1, 0):
        out = jnp.where(head_of_lane == r, cols[:, r:r + 1], out)
    return out


def _ssd_direction(x_ref, b_ref, c_ref, dt_ref, bias_ref, alog_ref, y_ref, st_sc, *, forward):
    li = lax.broadcasted_iota(jnp.int32, (CHUNK, CHUNK), 0)
    si = lax.broadcasted_iota(jnp.int32, (CHUNK, CHUNK), 1)
    keep = (li >= si) if forward else (li <= si)
    tri = jnp.where(keep, 1.0, 0.0).astype(BF16)
    head_of_lane = lax.broadcasted_iota(jnp.int32, (CHUNK, GROUP_WIDTH), 1) // SSM_HEADDIM

    dt = _softplus(dt_ref[...] + bias_ref[...])
    da = dt * (-jnp.exp(alog_ref[...]))
    cs = _split_dot(tri, da)
    tot = cs[CHUNK - 1:CHUNK, :] if forward else cs[0:1, :]
    cs_t = cs.T

    xf = x_ref[...].astype(F32)
    bm = b_ref[...]
    cm = c_ref[...]
    cb = lax.dot_general(cm, bm, (((1,), (1,)), ((), ())), preferred_element_type=F32)
    xdt = xf * _per_head_lanes(dt, head_of_lane)

    y = jnp.zeros((CHUNK, GROUP_WIDTH), F32)
    for r in range(SSM_HEADS_PER_GROUP):
        diff = cs[:, r:r + 1] - cs_t[r:r + 1, :]
        decay = jnp.exp(jnp.where(keep, diff, -jnp.inf))
        mat = (cb * decay).astype(BF16)
        xr = jnp.where(head_of_lane == r, xdt, 0.0).astype(BF16)
        y = y + jnp.dot(mat, xr, preferred_element_type=F32)

    st = st_sc[...]
    y = y + jnp.exp(_per_head_lanes(cs, head_of_lane)) * jnp.dot(
        cm, st.astype(BF16), preferred_element_type=F32)
    y_ref[...] = y.astype(y_ref.dtype)

    to_end = jnp.exp(_per_head_lanes(tot - cs, head_of_lane))
    bt = bm.astype(F32).T.astype(BF16)
    chunk_decay = jnp.exp(_per_head_lanes(tot, head_of_lane[0:1, :]))
    st_sc[...] = chunk_decay * st + jnp.dot(bt, (xdt * to_end).astype(BF16), preferred_element_type=F32)


def _ssd_kernel(xf_ref, bf_ref, cf_ref, dtf_ref, xb_ref, bb_ref, cb_ref, dtb_ref,
                biasf_ref, biasb_ref, alogf_ref, alogb_ref, yf_ref, yb_ref, stf_sc, stb_sc):
    @pl.when(pl.program_id(2) == 0)
    def _():
        stf_sc[...] = jnp.zeros(stf_sc.shape, F32)
        stb_sc[...] = jnp.zeros(stb_sc.shape, F32)

    _ssd_direction(xf_ref, bf_ref, cf_ref, dtf_ref, biasf_ref, alogf_ref, yf_ref, stf_sc, forward=True)
    _ssd_direction(xb_ref, bb_ref, cb_ref, dtb_ref, biasb_ref, alogb_ref, yb_ref, stb_sc, forward=False)


def _ssd_scan(xa, dt3, biasf, biasb, alogf, alogb):
    b, s, _ = xa.shape
    nc = s // CHUNK
    bcol = SSM_WIDTH // LANES
    ccol = bcol + SSM_GROUPS

    def fwd(col):
        return lambda bi, gi, ci: (bi, ci, col(gi))

    def bwd(col):
        return lambda bi, gi, ci: (bi, nc - 1 - ci, col(gi))

    def data_specs(order):
        return [
            pl.BlockSpec((None, CHUNK, GROUP_WIDTH), order(lambda gi: gi)),
            pl.BlockSpec((None, CHUNK, SSM_STATE), order(lambda gi: bcol + gi)),
            pl.BlockSpec((None, CHUNK, SSM_STATE), order(lambda gi: ccol + gi)),
            pl.BlockSpec((None, CHUNK, LANES), order(lambda gi: gi)),
        ]

    par_spec = pl.BlockSpec((None, 1, LANES), lambda bi, gi, ci: (gi, 0, 0))
    return pl.pallas_call(
        _ssd_kernel,
        grid=(b, SSM_GROUPS, nc),
        in_specs=data_specs(fwd) + data_specs(bwd) + [par_spec] * 4,
        out_specs=[
            pl.BlockSpec((None, CHUNK, GROUP_WIDTH), fwd(lambda gi: gi)),
            pl.BlockSpec((None, CHUNK, GROUP_WIDTH), bwd(lambda gi: gi)),
        ],
        out_shape=[jax.ShapeDtypeStruct((b, s, SSM_WIDTH), F32)] * 2,
        scratch_shapes=[pltpu.VMEM((SSM_STATE, GROUP_WIDTH), F32)] * 2,
        compiler_params=_cparams(("parallel", "parallel", "arbitrary")),
        name="ssd_scan",
    )(xa, xa, xa, dt3, xa, xa, xa, dt3, biasf, biasb, alogf, alogb)


def _layernorm(r, g, b):
    mu = jnp.mean(r, axis=-1, keepdims=True)
    d = r - mu
    var = jnp.mean(d * d, axis=-1, keepdims=True)
    return d * lax.rsqrt(var + LN_EPS) * g + b


def _silu(v):
    return v / (1.0 + jnp.exp(-v))


def _outproj_kernel(x_ref, attn_ref, yf_ref, yb_ref, xs_ref, z_ref, dsk_ref, nw_ref, w_ref,
                    g_ref, b_ref, o_ref):
    y = yf_ref[...] + yb_ref[...] + dsk_ref[...] * xs_ref[...].astype(F32)
    gated = y * _silu(z_ref[...].astype(F32))
    parts = []
    for gi in range(SSM_GROUPS):
        seg = gated[:, gi * GROUP_WIDTH:(gi + 1) * GROUP_WIDTH]
        ms = jnp.mean(seg * seg, axis=-1, keepdims=True)
        parts.append(seg * lax.rsqrt(ms + RMS_EPS))
    ssd = (jnp.concatenate(parts, axis=1) * nw_ref[...]).astype(BF16)
    mix = (jnp.dot(attn_ref[...], w_ref[0:ATTN_WIDTH, :], preferred_element_type=F32)
           + jnp.dot(ssd, w_ref[ATTN_WIDTH:, :], preferred_element_type=F32))
    o_ref[...] = _layernorm(DEEPNORM_ALPHA * x_ref[...] + mix, g_ref[...], b_ref[...])


def _outproj(x2d, attn2d, yf2d, yb2d, xa2d, h2d, dsk_row, nw_row, w_out, g_row, b_row):
    n = x2d.shape[0]
    tm = min(512, n)
    row = lambda i: (i, 0)
    const = lambda i: (0, 0)
    zcol = Z_COL0 // SSM_WIDTH
    return pl.pallas_call(
        _outproj_kernel,
        grid=(n // tm,),
        in_specs=[
            pl.BlockSpec((tm, D_MODEL), row),
            pl.BlockSpec((tm, ATTN_WIDTH), row),
            pl.BlockSpec((tm, SSM_WIDTH), row),
            pl.BlockSpec((tm, SSM_WIDTH), row),
            pl.BlockSpec((tm, SSM_WIDTH), row),
            pl.BlockSpec((tm, SSM_WIDTH), lambda i: (i, zcol)),
            pl.BlockSpec((1, SSM_WIDTH), const),
            pl.BlockSpec((1, SSM_WIDTH), const),
            pl.BlockSpec((ATTN_WIDTH + SSM_WIDTH, D_MODEL), const),
            pl.BlockSpec((1, D_MODEL), const),
            pl.BlockSpec((1, D_MODEL), const),
        ],
        out_specs=pl.BlockSpec((tm, D_MODEL), row),
        out_shape=jax.ShapeDtypeStruct((n, D_MODEL), F32),
        compiler_params=_cparams(("parallel",)),
        name="gate_outproj_ln",
    )(x2d, attn2d, yf2d, yb2d, xa2d, h2d, dsk_row, nw_row, w_out, g_row, b_row)


def _ffn_kernel(x_ref, wg_ref, wu_ref, wd_ref, g_ref, b_ref, o_ref, acc_sc, *, nj):
    j = pl.program_id(1)
    xb = x_ref[...].astype(BF16)
    gate = jnp.dot(xb, wg_ref[...], preferred_element_type=F32)
    up = jnp.dot(xb, wu_ref[...], preferred_element_type=F32)
    part = jnp.dot((_silu(gate) * up).astype(BF16), wd_ref[...], preferred_element_type=F32)

    @pl.when(j == 0)
    def _():
        acc_sc[...] = part

    @pl.when(j > 0)
    def _():
        acc_sc[...] = acc_sc[...] + part

    @pl.when(j == nj - 1)
    def _():
        o_ref[...] = _layernorm(DEEPNORM_ALPHA * x_ref[...] + acc_sc[...], g_ref[...], b_ref[...])


def _ffn(x2d, wg, wu, wd, g_row, b_row):
    n = x2d.shape[0]
    tm = min(1024, n)
    nj = 2
    tf = D_FF // nj
    kern = functools.partial(_ffn_kernel, nj=nj)
    return pl.pallas_call(
        kern,
        grid=(n // tm, nj),
        in_specs=[
            pl.BlockSpec((tm, D_MODEL), lambda i, j: (i, 0)),
            pl.BlockSpec((D_MODEL, tf), lambda i, j: (0, j)),
            pl.BlockSpec((D_MODEL, tf), lambda i, j: (0, j)),
            pl.BlockSpec((tf, D_MODEL), lambda i, j: (j, 0)),
            pl.BlockSpec((1, D_MODEL), lambda i, j: (0, 0)),
            pl.BlockSpec((1, D_MODEL), lambda i, j: (0, 0)),
        ],
        out_specs=pl.BlockSpec((tm, D_MODEL), lambda i, j: (i, 0)),
        out_shape=jax.ShapeDtypeStruct((n, D_MODEL), F32),
        scratch_shapes=[pltpu.VMEM((tm, D_MODEL), F32)],
        compiler_params=_cparams(("parallel", "arbitrary")),
        name="swiglu_ln",
    )(x2d, wg, wu, wd, g_row, b_row)


def _group_lanes(v):
    g = v.astype(F32).reshape(SSM_GROUPS, 1, SSM_HEADS_PER_GROUP)
    return jnp.pad(g, ((0, 0), (0, 0), (0, LANES - SSM_HEADS_PER_GROUP)))


def kernel(x, w_in, lambda_q1, lambda_k1, lambda_q2, lambda_k2, subln_w, conv_w, conv_b, dt_bias_fwd, dt_bias_bwd, a_log_fwd, a_log_bwd, d_skip, ssm_norm_w, w_out, ln1_g, ln1_b, w_gate, w_up, w_down, ln2_g, ln2_b):
    b, s, d = x.shape
    n = b * s
    tq = min(512, s)
    tk = min(512, s)
    nk = s // tk
    neg_slopes = -LOG2E * jnp.exp2(-jnp.arange(1, N_HEADS_ATTN + 1, dtype=F32))
    neg_slopes = jnp.broadcast_to(neg_slopes[:, None, None], (N_HEADS_ATTN, 1, LANES))
    x2d = x.reshape(n, d)
    for l in range(DEPTH):
        lambda_init = 0.8 - 0.6 * math.exp(-0.3 * l)
        wl = w_in[l]
        w_main = jnp.concatenate(
            [wl[:, :ATTN_WIDTH] * (HEAD_DIM_ATTN ** -0.5 * LOG2E), wl[:, ATTN_WIDTH:MAIN_COLS]], axis=1
        ).astype(BF16)
        w_dt = wl[:, MAIN_COLS:].reshape(d, SSM_GROUPS, SSM_HEADS_PER_GROUP)
        w_dt = jnp.pad(w_dt, ((0, 0), (0, 0), (0, LANES - SSM_HEADS_PER_GROUP)))
        w_dt = w_dt.reshape(d, SSM_GROUPS * LANES).astype(BF16)
        h2d, dt2d = _inproj(x2d, w_main, w_dt)
        h3 = h2d.reshape(b, s, MAIN_COLS)

        q4 = h3[:, :, :ATTN_WIDTH].reshape(b, s, N_HEADS_ATTN, 2 * HEAD_DIM_ATTN)
        qT = q4.transpose(0, 2, 3, 1)
        v5 = h3[:, :, 2 * ATTN_WIDTH:3 * ATTN_WIDTH].reshape(b, nk, tk, N_HEADS_ATTN, V_DIM_ATTN)
        vT = v5.transpose(0, 3, 1, 4, 2)
        lam_params = jnp.stack([lambda_q1[l], lambda_k1[l], lambda_q2[l], lambda_k2[l]]).astype(F32)
        attn = _attention(h3, qT, vT, lam_params, subln_w[l].astype(F32)[:, None], neg_slopes,
                          lambda_init, tq, tk)

        xa = _conv_silu(h3, conv_w[l].astype(F32), conv_b[l].astype(F32)[None, :])
        yf, yb = _ssd_scan(xa, dt2d.reshape(b, s, SSM_GROUPS * LANES),
                           _group_lanes(dt_bias_fwd[l]), _group_lanes(dt_bias_bwd[l]),
                           _group_lanes(a_log_fwd[l]), _group_lanes(a_log_bwd[l]))

        dsk_row = jnp.repeat(d_skip[l].astype(F32), SSM_HEADDIM)[None, :]
        x2d = _outproj(x2d, attn.reshape(n, ATTN_WIDTH), yf.reshape(n, SSM_WIDTH), yb.reshape(n, SSM_WIDTH),
                       xa.reshape(n, CONV_CH), h2d, dsk_row, ssm_norm_w[l].astype(F32)[None, :],
                       w_out[l].astype(BF16), ln1_g[l].astype(F32)[None, :], ln1_b[l].astype(F32)[None, :])
        x2d = _ffn(x2d, w_gate[l].astype(BF16), w_up[l].astype(BF16), w_down[l].astype(BF16),
                   ln2_g[l].astype(F32)[None, :], ln2_b[l].astype(F32)[None, :])
    return x2d.reshape(b, s, d)
```
